```python
import math
import jax
import jax.numpy as jnp
from jax import lax
import numpy as np

D_MODEL = 2048
BATCH = 8
SEQ = 4096
DEPTH = 4

CHUNK = 64
N_MEM = 256
EPS = 1e-6

HG_HEADS = 4
HG_DK = 128
HG_DV = 128
HG_QK = HG_HEADS * HG_DK
HG_W = HG_HEADS * HG_DV

RET_HEADS = 4
RET_DK = 128
RET_DV = 128
RET_QK = RET_HEADS * RET_DK
RET_W = RET_HEADS * RET_DV
ROPE_BASE = 10000.0

SSM_HEADS = 16
SSM_HEADDIM = 64
SSM_W = SSM_HEADS * SSM_HEADDIM
SSM_STATE = 128
SSM_GROUPS = 4
SSM_HPG = SSM_HEADS // SSM_GROUPS
SSM_CONV = 4
SSM_CONV_CH = SSM_W + 2 * SSM_GROUPS * SSM_STATE

MIX_W = HG_W + RET_W + SSM_W
MIX_SPLITS = (HG_QK, HG_QK, HG_W, HG_W, RET_QK, RET_QK, RET_W, RET_W, SSM_W, SSM_CONV_CH, SSM_HEADS)
IN_COLS = 2 * HG_QK + 2 * HG_W + 2 * RET_QK + 2 * RET_W + SSM_W + SSM_CONV_CH + SSM_HEADS

XA_HEADS = 4
XA_HEADDIM = 128
XA_W = XA_HEADS * XA_HEADDIM

FFN_DIM = 5632
FFN_CONV = 3

kernel_name = 'hybrid_hgrn2_retnet_mamba2_stream_encoder'


def rmsnorm(x, w):
    xf = x.astype(jnp.float32)
    y = xf * lax.rsqrt(jnp.mean(xf * xf, axis=-1, keepdims=True) + EPS)
    return (y * w.astype(jnp.float32)).astype(x.dtype)


def grouped_rmsnorm(x, w, n_groups):
    shp = x.shape
    xf = x.astype(jnp.float32).reshape(shp[:-1] + (n_groups, shp[-1] // n_groups))
    y = xf * lax.rsqrt(jnp.mean(xf * xf, axis=-1, keepdims=True) + EPS)
    return (y.reshape(shp) * w.astype(jnp.float32)).astype(x.dtype)


def causal_dwconv(x, w, b):
    K, C = w.shape
    y = lax.conv_general_dilated(x, w[:, None, :].astype(x.dtype), window_strides=(1,),
                                 padding=((K - 1, 0),), dimension_numbers=('NWC', 'WIO', 'NWC'),
                                 feature_group_count=C)
    return y + b.astype(x.dtype)


def rotary(x):
    T, Dh = x.shape[1], x.shape[-1]
    half = Dh // 2
    inv = ROPE_BASE ** (-jnp.arange(half, dtype=jnp.float32) / half)
    ang = jnp.arange(T, dtype=jnp.float32)[:, None] * inv[None, :]
    cos = jnp.cos(ang)[None, :, None, :].astype(x.dtype)
    sin = jnp.sin(ang)[None, :, None, :].astype(x.dtype)
    x1, x2 = x[..., :half], x[..., half:]
    return jnp.concatenate([x1 * cos - x2 * sin, x1 * sin + x2 * cos], axis=-1)


def chunked_gated_recurrence(q, k, v, log_f):
    Bn, T, H, K = q.shape
    V = v.shape[-1]
    N = T // CHUNK

    def to_chunks(a):
        return a.reshape(Bn, N, CHUNK, H, a.shape[-1]).transpose(1, 0, 3, 2, 4)

    causal = jnp.tril(jnp.ones((CHUNK, CHUNK), dtype=bool))

    def step(S, inp):
        qn, kn, vn, gn = inp
        b = jnp.cumsum(gn, axis=2)
        seg = jnp.where(causal[:, :, None], b[:, :, :, None, :] - b[:, :, None, :, :], -jnp.inf)
        A = jnp.einsum('bhck,bhsk,bhcsk->bhcs', qn, kn, jnp.exp(seg).astype(qn.dtype))
        o = (jnp.einsum('bhcs,bhsv->bhcv', A, vn)
             + jnp.einsum('bhck,bhkv->bhcv', qn * jnp.exp(b).astype(qn.dtype), S))
        b_last = b[:, :, -1:]
        S = (jnp.exp(b_last[:, :, 0])[..., None].astype(S.dtype) * S
             + jnp.einsum('bhsk,bhsv->bhkv', kn * jnp.exp(b_last - b).astype(kn.dtype), vn))
        return S, o

    S0 = jnp.zeros((Bn, H, K, V), q.dtype)
    _, o = lax.scan(step, S0, (to_chunks(q), to_chunks(k), to_chunks(v), to_chunks(log_f)))
    return o.transpose(1, 0, 3, 2, 4).reshape(Bn, T, H, V)


def chunked_decay_recurrence(q, k, v, log_a):
    Bn, T, G, K = q.shape
    R, V = v.shape[3], v.shape[4]
    N = T // CHUNK
    qc = q.reshape(Bn, N, CHUNK, G, K)
    kc = k.reshape(Bn, N, CHUNK, G, K)
    vc = v.reshape(Bn, N, CHUNK, G, R, V)
    cum = jnp.cumsum(log_a.astype(jnp.float32).reshape(Bn, N, CHUNK, G, R), axis=2)
    cum_h = cum.transpose(0, 1, 3, 4, 2)
    causal = jnp.tril(jnp.ones((CHUNK, CHUNK), dtype=bool))
    seg = jnp.where(causal, cum_h[..., :, None] - cum_h[..., None, :], -jnp.inf)
    decay = jnp.exp(seg).astype(q.dtype)
    scores = jnp.einsum('bncgk,bnsgk->bngcs', qc, kc)
    o_intra = jnp.einsum('bngcs,bngrcs,bnsgrv->bncgrv', scores, decay, vc)
    w_end = jnp.exp(cum[:, :, -1:] - cum).astype(q.dtype)
    U = jnp.einsum('bnsgk,bnsgr,bnsgrv->bngrkv', kc, w_end, vc)
    chunk_decay = jnp.exp(cum[:, :, -1]).astype(q.dtype)

    def step(S, inp):
        U_n, d_n = inp
        return d_n[..., None, None] * S + U_n, S

    S0 = jnp.zeros((Bn, G, R, K, V), q.dtype)
    _, S_in = lax.scan(step, S0, (jnp.moveaxis(U, 1, 0), jnp.moveaxis(chunk_decay, 1, 0)))
    S_in = jnp.moveaxis(S_in, 0, 1)
    o_inter = jnp.einsum('bncgk,bncgr,bngrkv->bncgrv', qc, jnp.exp(cum).astype(q.dtype), S_in)
    return (o_intra + o_inter).reshape(Bn, T, G, R, V)


def hybrid_mixer(h, lb, w_in, w_out, hg_norm, ret_norm, conv_w, conv_b, dt_bias, A_log, D_skip, ssm_norm):
    Bn, T, _ = h.shape
    proj = h @ w_in
    split_idx = [int(s) for s in np.cumsum(MIX_SPLITS)[:-1]]
    hq, hf, hi, hg, rq, rk, rv, rg, z, xbc, dt = jnp.split(proj, split_idx, axis=-1)

    hf32 = hf.astype(jnp.float32)
    log_f = jnp.log(lb + (1.0 - lb) * jax.nn.sigmoid(hf32))
    k_hg = ((1.0 - lb) * jax.nn.sigmoid(-hf32)).astype(h.dtype)
    q_hg = jax.nn.silu(hq)
    o_hg = chunked_gated_recurrence(q_hg.reshape(Bn, T, HG_HEADS, HG_DK), k_hg.reshape(Bn, T, HG_HEADS, HG_DK),
                                    hi.reshape(Bn, T, HG_HEADS, HG_DV), log_f.reshape(Bn, T, HG_HEADS, HG_DK))
    hg_out = grouped_rmsnorm(o_hg.reshape(Bn, T, HG_W), hg_norm, HG_HEADS) * jax.nn.sigmoid(hg)

    log_gamma = jnp.log(1.0 - 2.0 ** (-5.0 - jnp.arange(RET_HEADS, dtype=jnp.float32)))
    q_r = rotary(rq.reshape(Bn, T, RET_HEADS, RET_DK))
    k_r = rotary(rk.reshape(Bn, T, RET_HEADS, RET_DK)) * (RET_DK ** -0.5)
    log_a_r = jnp.broadcast_to(log_gamma[None, None, :, None], (Bn, T, RET_HEADS, 1))
    o_r = chunked_decay_recurrence(q_r, k_r, rv.reshape(Bn, T, RET_HEADS, 1, RET_DV), log_a_r)
    ret_out = grouped_rmsnorm(o_r.reshape(Bn, T, RET_W), ret_norm, RET_HEADS) * jax.nn.silu(rg)

    xbc = jax.nn.silu(causal_dwconv(xbc, conv_w, conv_b))
    xs = xbc[..., :SSM_W]
    Bm = xbc[..., SSM_W:SSM_W + SSM_GROUPS * SSM_STATE].reshape(Bn, T, SSM_GROUPS, SSM_STATE)
    Cm = xbc[..., SSM_W + SSM_GROUPS * SSM_STATE:].reshape(Bn, T, SSM_GROUPS, SSM_STATE)
    dt_s = jax.nn.softplus(dt.astype(jnp.float32) + dt_bias.astype(jnp.float32))
    A = -jnp.exp(A_log.astype(jnp.float32))
    log_a_s = (dt_s * A).reshape(Bn, T, SSM_GROUPS, SSM_HPG)
    xh = xs.reshape(Bn, T, SSM_GROUPS, SSM_HPG, SSM_HEADDIM)
    v_s = xh * dt_s.reshape(Bn, T, SSM_GROUPS, SSM_HPG, 1).astype(xh.dtype)
    y = chunked_decay_recurrence(Cm, Bm, v_s, log_a_s)
    y = y + D_skip.reshape(SSM_GROUPS, SSM_HPG, 1).astype(xh.dtype) * xh
    ssm_out = grouped_rmsnorm(y.reshape(Bn, T, SSM_W) * jax.nn.silu(z), ssm_norm, SSM_GROUPS)

    return jnp.concatenate([hg_out, ret_out, ssm_out], axis=-1) @ w_out


def memory_cross_attention(h, mem_n, wq, wkv, wo):
    Bn, T, _ = h.shape
    q = (h @ wq).reshape(Bn, T, XA_HEADS, XA_HEADDIM)
    kv = mem_n @ wkv
    k = kv[..., :XA_W].reshape(Bn, N_MEM, XA_HEADS, XA_HEADDIM)
    v = kv[..., XA_W:].reshape(Bn, N_MEM, XA_HEADS, XA_HEADDIM)
    s = jnp.einsum('bthd,bmhd->bhtm', q, k).astype(jnp.float32) * (XA_HEADDIM ** -0.5)
    p = jax.nn.softmax(s, axis=-1).astype(v.dtype)
    o = jnp.einsum('bhtm,bmhd->bthd', p, v).reshape(Bn, T, XA_W)
    return o @ wo


def conv_glu_ffn(h, w_up, conv_w, conv_b, w_down):
    u = causal_dwconv(h @ w_up, conv_w, conv_b)
    gate, up = u[..., :FFN_DIM], u[..., FFN_DIM:]
    return (jax.nn.silu(gate) * up) @ w_down


def setup_inputs(seed: int = 0) -> dict:
    key = jax.random.key(seed)
    ks = jax.random.split(key, 26)
    f32 = jnp.float32

    def nrm(k, shape, scale):
        return jax.random.normal(k, shape, f32) * scale

    def gain(k, shape):
        return 1.0 + 0.02 * jax.random.normal(k, shape, f32)

    dt0 = jnp.exp(jax.random.uniform(ks[9], (DEPTH, SSM_HEADS), f32, math.log(1e-3), math.log(1e-1)))
    return {
        'x': nrm(ks[0], (BATCH, SEQ, D_MODEL), 1.0),
        'mem': nrm(ks[1], (BATCH, N_MEM, D_MODEL), 1.0),
        'w_in': nrm(ks[2], (DEPTH, D_MODEL, IN_COLS), D_MODEL ** -0.5),
        'w_out': nrm(ks[3], (DEPTH, MIX_W, D_MODEL), MIX_W ** -0.5),
        'hg_lb_logits': nrm(ks[4], (DEPTH, HG_QK), 0.5),
        'hg_norm': gain(ks[5], (DEPTH, HG_W)),
        'ret_norm': gain(ks[6], (DEPTH, RET_W)),
        'ssm_conv_w': nrm(ks[7], (DEPTH, SSM_CONV, SSM_CONV_CH), SSM_CONV ** -0.5),
        'ssm_conv_b': nrm(ks[8], (DEPTH, SSM_CONV_CH), 0.01),
        'ssm_dt_bias': dt0 + jnp.log(-jnp.expm1(-dt0)),
        'ssm_A_log': jnp.log(jax.random.uniform(ks[10], (DEPTH, SSM_HEADS), f32, 1.0, 16.0)),
        'ssm_D': 1.0 + 0.1 * jax.random.normal(ks[11], (DEPTH, SSM_HEADS), f32),
        'ssm_norm': gain(ks[12], (DEPTH, SSM_W)),
        'norm_mix': gain(ks[13], (DEPTH, D_MODEL)),
        'norm_xattn': gain(ks[14], (DEPTH, D_MODEL)),
        'norm_mem': gain(ks[15], (DEPTH, D_MODEL)),
        'xa_wq': nrm(ks[16], (DEPTH, D_MODEL, XA_W), D_MODEL ** -0.5),
        'xa_wkv': nrm(ks[17], (DEPTH, D_MODEL, 2 * XA_W), D_MODEL ** -0.5),
        'xa_wo': nrm(ks[18], (DEPTH, XA_W, D_MODEL), XA_W ** -0.5),
        'norm_ffn': gain(ks[19], (DEPTH, D_MODEL)),
        'ffn_w_up': nrm(ks[20], (DEPTH, D_MODEL, 2 * FFN_DIM), D_MODEL ** -0.5),
        'ffn_conv_w': nrm(ks[21], (DEPTH, FFN_CONV, 2 * FFN_DIM), FFN_CONV ** -0.5),
        'ffn_conv_b': nrm(ks[22], (DEPTH, 2 * FFN_DIM), 0.01),
        'ffn_w_down': nrm(ks[23], (DEPTH, FFN_DIM, D_MODEL), FFN_DIM ** -0.5),
        'norm_final': gain(ks[24], (D_MODEL,)),
    }


def reference(x, mem, w_in, w_out, hg_lb_logits, hg_norm, ret_norm, ssm_conv_w, ssm_conv_b, ssm_dt_bias,
              ssm_A_log, ssm_D, ssm_norm, norm_mix, norm_xattn, norm_mem, xa_wq, xa_wkv, xa_wo, norm_ffn,
              ffn_w_up, ffn_conv_w, ffn_conv_b, ffn_w_down, norm_final):
    p = jax.nn.softmax(hg_lb_logits.astype(jnp.float32), axis=0)
    lower_bounds = jnp.cumsum(p, axis=0) - p[0]
    for l in range(DEPTH):
        x = x + hybrid_mixer(rmsnorm(x, norm_mix[l]), lower_bounds[l], w_in[l], w_out[l], hg_norm[l],
                             ret_norm[l], ssm_conv_w[l], ssm_conv_b[l], ssm_dt_bias[l], ssm_A_log[l],
                             ssm_D[l], ssm_norm[l])
        x = x + memory_cross_attention(rmsnorm(x, norm_xattn[l]), rmsnorm(mem, norm_mem[l]),
                                       xa_wq[l], xa_wkv[l], xa_wo[l])
        x = x + conv_glu_ffn(rmsnorm(x, norm_ffn[l]), ffn_w_up[l], ffn_conv_w[l], ffn_conv_b[l], ffn_w_down[l])
    return rmsnorm(x, norm_final)
```

```python
import functools
import math

import jax
import jax.numpy as jnp
from jax import lax
from jax.experimental import pallas as pl
from jax.experimental.pallas import tpu as pltpu

F32 = jnp.float32
BF16 = jnp.bfloat16
EPS = 1e-6
LANES = 128
SUBLANES = 8
VMEM_LIMIT_BYTES = 56 * 1024 * 1024

HEAD_DIM = 128
HG_HEADS = 4
RET_HEADS = 4
SSM_HEADS = 16
SSM_HEADDIM = 64
SSM_GROUPS = 4
XA_HEADS = 4
ROPE_BASE = 10000.0
HG_CHUNK = 64
HG_SUB = 16


def _params(*sem):
    return pltpu.CompilerParams(dimension_semantics=sem, vmem_limit_bytes=VMEM_LIMIT_BYTES)


def _rms(x, gain):
    ms = jnp.mean(x * x, axis=-1, keepdims=True)
    return x * lax.rsqrt(ms + EPS) * gain


def _dot(a, b):
    return jnp.dot(a, b, preferred_element_type=F32)


def _dot_nt(a, b):
    return lax.dot_general(a, b, (((1,), (1,)), ((), ())), preferred_element_type=F32)


def _dot_tn(a, b):
    return lax.dot_general(a, b, (((0,), (0,)), ((), ())), preferred_element_type=F32)


def _sigmoid(x):
    return 1.0 / (1.0 + jnp.exp(-x))


def _silu(x):
    return x * _sigmoid(x)


def _cumsum_rows(x, tri):
    hi = x.astype(BF16)
    lo = (x - hi.astype(F32)).astype(BF16)
    return _dot(tri, hi) + _dot(tri, lo)


def _tri(n):
    r = lax.broadcasted_iota(jnp.int32, (n, n), 0)
    c = lax.broadcasted_iota(jnp.int32, (n, n), 1)
    return (r >= c).astype(BF16)


def _norm_matmul_kernel(x_ref, g_ref, w_ref, o_ref, h_ref):
    @pl.when(pl.program_id(1) == 0)
    def _():
        h_ref[...] = _rms(x_ref[...], g_ref[...]).astype(BF16)

    o_ref[...] = _dot(h_ref[...], w_ref[...]).astype(o_ref.dtype)


def norm_matmul(x, gain, w, out_dtype, tm, tn):
    n, d = x.shape
    nout = w.shape[1]
    tm, tn = min(tm, n), min(tn, nout)
    return pl.pallas_call(
        _norm_matmul_kernel,
        grid=(n // tm, nout // tn),
        in_specs=[pl.BlockSpec((tm, d), lambda i, j: (i, 0)),
                  pl.BlockSpec((1, d), lambda i, j: (0, 0)),
                  pl.BlockSpec((d, tn), lambda i, j: (0, j))],
        out_specs=pl.BlockSpec((tm, tn), lambda i, j: (i, j)),
        out_shape=jax.ShapeDtypeStruct((n, nout), out_dtype),
        scratch_shapes=[pltpu.VMEM((tm, d), BF16)],
        compiler_params=_params("arbitrary", "arbitrary"),
        name="norm_matmul",
    )(x, gain.reshape(1, d), w)


def _resid_matmul_kernel(x_ref, a0_ref, a1_ref, a2_ref, w_ref, o_ref):
    k0, k1 = a0_ref.shape[1], a1_ref.shape[1]
    acc = x_ref[...] + _dot(a0_ref[...], w_ref[0:k0, :])
    acc = acc + _dot(a1_ref[...], w_ref[k0:k0 + k1, :])
    acc = acc + _dot(a2_ref[...], w_ref[k0 + k1:, :])
    o_ref[...] = acc


def resid_matmul(x, a0, a1, a2, w, tm):
    n, d = x.shape
    tm = min(tm, n)
    ktot = w.shape[0]
    acts = (a0, a1, a2)
    return pl.pallas_call(
        _resid_matmul_kernel,
        grid=(n // tm,),
        in_specs=[pl.BlockSpec((tm, d), lambda i: (i, 0))]
        + [pl.BlockSpec((tm, a.shape[1]), lambda i: (i, 0)) for a in acts]
        + [pl.BlockSpec((ktot, d), lambda i: (0, 0))],
        out_specs=pl.BlockSpec((tm, d), lambda i: (i, 0)),
        out_shape=jax.ShapeDtypeStruct((n, d), F32),
        compiler_params=_params("arbitrary"),
        name="mix_out_proj",
    )(x, a0, a1, a2, w)


def _hgrn_kernel(hq_ref, hi_ref, hg_ref, hf_ref, lb_ref, gn_ref, o_ref, st_ref, y_ref, *, nheads):
    C, SB = HG_CHUNK, HG_SUB
    nsub = C // SB

    @pl.when(pl.program_id(1) == 0)
    def _():
        st_ref[...] = jnp.zeros_like(st_ref)

    nchunks = hq_ref.shape[0] // C
    tri = _tri(C)
    ones = jnp.ones((LANES, LANES), BF16)
    sub_row = lax.broadcasted_iota(jnp.int32, (SB, LANES), 0)

    for h in range(nheads):
        cs = slice(h * HEAD_DIM, (h + 1) * HEAD_DIM)
        lb = lb_ref[:, cs]
        gn = gn_ref[:, cs]

        def body(n, st, cs=cs, lb=lb, gn=gn):
            rows = pl.ds(pl.multiple_of(n * C, C), C)
            hf = hf_ref[rows, cs]
            log_f = jnp.log(lb + (1.0 - lb) * _sigmoid(hf))
            k = (1.0 - lb) * _sigmoid(-hf)
            q = _silu(hq_ref[rows, cs].astype(F32))
            v = hi_ref[rows, cs].astype(F32)
            v16 = v.astype(BF16)
            b = _cumsum_rows(log_f, tri)
            b_last = b[C - 1:C, :]

            o = _dot_nt((q * jnp.exp(b)).astype(BF16), st.astype(BF16))
            kw = k * jnp.exp(b_last - b)
            st_new = st * jnp.exp(b_last) + _dot_tn(v16, kw.astype(BF16))

            def off_block(q0, q1, k0, k1):
                ref = b[q0 - 1:q0, :]
                qt = q[q0:q1] * jnp.exp(b[q0:q1] - ref)
                kt = k[k0:k1] * jnp.exp(ref - b[k0:k1])
                a = _dot_nt(qt.astype(BF16), kt.astype(BF16))
                return _dot(a.astype(BF16), v16[k0:k1])

            half = C // 2
            o_lo = [jnp.zeros((SB, HEAD_DIM), F32), off_block(SB, 2 * SB, 0, SB)]
            o_hi = off_block(half, C, 0, half)
            o_hi = [o_hi[0:SB], o_hi[SB:2 * SB] + off_block(3 * SB, C, half, 3 * SB)]
            o = o + jnp.concatenate(o_lo + o_hi, axis=0)

            for j in range(nsub):
                r0 = j * SB
                qj, bj = q[r0:r0 + SB], b[r0:r0 + SB]
                for s in range(SB):
                    e = jnp.exp(jnp.minimum(bj - b[r0 + s:r0 + s + 1, :], 0.0))
                    y = jnp.where(sub_row >= s, qj * k[r0 + s:r0 + s + 1, :] * e, 0.0)
                    y_ref[pl.ds((r0 + s) * SB, SB), :] = y.astype(BF16)
            red = _dot(y_ref[...], ones)
            o_diag = []
            for j in range(nsub):
                r0 = j * SB
                acc = jnp.zeros((SB, HEAD_DIM), F32)
                for s in range(SB):
                    acc = acc + red[(r0 + s) * SB:(r0 + s + 1) * SB, :] * v[r0 + s:r0 + s + 1, :]
                o_diag.append(acc)
            o = o + jnp.concatenate(o_diag, axis=0)

            g = hg_ref[rows, cs].astype(F32)
            o_ref[rows, cs] = (_rms(o, gn) * _sigmoid(g)).astype(o_ref.dtype)
            return st_new

        st_ref[h] = lax.fori_loop(0, nchunks, body, st_ref[h])


def hgrn_mixer(p16, g32, lb, gain, batch, seq, tt, cb_q, cb_i, cb_g):
    n = p16.shape[0]
    w = HG_HEADS * HEAD_DIM
    tt = min(tt, seq)
    tps = seq // tt
    row = lambda b, t: b * tps + t
    return pl.pallas_call(
        functools.partial(_hgrn_kernel, nheads=HG_HEADS),
        grid=(batch, tps),
        in_specs=[pl.BlockSpec((tt, w), lambda b, t: (row(b, t), cb_q)),
                  pl.BlockSpec((tt, w), lambda b, t: (row(b, t), cb_i)),
                  pl.BlockSpec((tt, w), lambda b, t: (row(b, t), cb_g)),
                  pl.BlockSpec((tt, w), lambda b, t: (row(b, t), 0)),
                  pl.BlockSpec((1, w), lambda b, t: (0, 0)),
                  pl.BlockSpec((1, w), lambda b, t: (0, 0))],
        out_specs=pl.BlockSpec((tt, w), lambda b, t: (row(b, t), 0)),
        out_shape=jax.ShapeDtypeStruct((n, w), BF16),
        scratch_shapes=[pltpu.VMEM((HG_HEADS, HEAD_DIM, HEAD_DIM), F32),
                        pltpu.VMEM((HG_CHUNK * HG_SUB, LANES), BF16)],
        compiler_params=_params("arbitrary", "arbitrary"),
        name="hgrn_mixer",
    )(p16, p16, p16, g32, lb.reshape(1, w), gain.reshape(1, w))


def _ret_kernel(q_ref, k_ref, v_ref, g_ref, cos_ref, sin_ref, gn_ref, o_ref, s_ref, *, nheads):
    @pl.when(pl.program_id(1) == 0)
    def _():
        s_ref[...] = jnp.zeros_like(s_ref)

    C = q_ref.shape[0]
    ci = lax.broadcasted_iota(jnp.int32, (C, C), 0)
    si = lax.broadcasted_iota(jnp.int32, (C, C), 1)
    causal = ci >= si
    dpos = (ci - si).astype(F32)
    rowpos = lax.broadcasted_iota(jnp.int32, (C, HEAD_DIM), 0).astype(F32)
    cosf = cos_ref[...]
    sinf = sin_ref[...]

    def rot(x):
        return x * cosf + pltpu.roll(x, HEAD_DIM // 2, 1) * sinf

    for h in range(nheads):
        lg = math.log(1.0 - 2.0 ** (-5.0 - h))
        cs = slice(h * HEAD_DIM, (h + 1) * HEAD_DIM)
        qr = rot(q_ref[:, cs].astype(F32))
        kr = rot(k_ref[:, cs].astype(F32)) * (HEAD_DIM ** -0.5)
        v = v_ref[:, cs]
        scores = _dot_nt(qr.astype(BF16), kr.astype(BF16))
        decay = jnp.where(causal, jnp.exp(dpos * lg), 0.0)
        o = _dot((scores * decay).astype(BF16), v)
        s_in = s_ref[h]
        o = o + _dot((qr * jnp.exp((rowpos + 1.0) * lg)).astype(BF16), s_in.astype(BF16))
        kw = kr * jnp.exp((C - 1.0 - rowpos) * lg)
        s_ref[h] = s_in * math.exp(C * lg) + _dot_tn(kw.astype(BF16), v)
        g = g_ref[:, cs].astype(F32)
        o_ref[:, cs] = (_rms(o, gn_ref[:, cs]) * _silu(g)).astype(o_ref.dtype)


def ret_mixer(p16, cosf, sinf, gain, batch, seq, chunk, cb_q, cb_k, cb_v, cb_g):
    n = p16.shape[0]
    w = RET_HEADS * HEAD_DIM
    chunk = min(chunk, seq)
    tps = seq // chunk
    row = lambda b, t: b * tps + t
    return pl.pallas_call(
        functools.partial(_ret_kernel, nheads=RET_HEADS),
        grid=(batch, tps),
        in_specs=[pl.BlockSpec((chunk, w), lambda b, t: (row(b, t), cb_q)),
                  pl.BlockSpec((chunk, w), lambda b, t: (row(b, t), cb_k)),
                  pl.BlockSpec((chunk, w), lambda b, t: (row(b, t), cb_v)),
                  pl.BlockSpec((chunk, w), lambda b, t: (row(b, t), cb_g)),
                  pl.BlockSpec((chunk, HEAD_DIM), lambda b, t: (t, 0)),
                  pl.BlockSpec((chunk, HEAD_DIM), lambda b, t: (t, 0)),
                  pl.BlockSpec((1, w), lambda b, t: (0, 0))],
        out_specs=pl.BlockSpec((chunk, w), lambda b, t: (row(b, t), 0)),
        out_shape=jax.ShapeDtypeStruct((n, w), BF16),
        scratch_shapes=[pltpu.VMEM((RET_HEADS, HEAD_DIM, HEAD_DIM), F32)],
        compiler_params=_params("arbitrary", "arbitrary"),
        name="ret_mixer",
    )(p16, p16, p16, p16, cosf, sinf, gain.reshape(1, w))


def _ssd_kernel(xbc_ref, z_ref, dt_ref, cw_ref, cb_ref, dtb_ref, alog_ref, dsk_ref, gn_ref, o_ref,
                ext_ref, tail_ref, xa_ref, s_ref, y_ref):
    C, W = xbc_ref.shape
    kconv = cw_ref.shape[0]
    hpg = SSM_HEADS // SSM_GROUPS
    xw = SSM_HEADS * SSM_HEADDIM
    gw = hpg * SSM_HEADDIM
    pad = SUBLANES

    @pl.when(pl.program_id(1) == 0)
    def _():
        tail_ref[...] = jnp.zeros_like(tail_ref)
        s_ref[...] = jnp.zeros_like(s_ref)

    ext_ref[0:pad, :] = tail_ref[...]
    ext_ref[pad:pad + C, :] = xbc_ref[...].astype(F32)
    tail_ref[...] = ext_ref[C:C + pad, :]
    cblk = 4 * LANES
    for c0 in range(0, W, cblk):
        cc = slice(c0, c0 + cblk)
        acc = cb_ref[:, cc] + cw_ref[kconv - 1:kconv, cc] * ext_ref[pad:pad + C, cc]
        for j in range(kconv - 1):
            off = pad - (kconv - 1) + j
            acc = acc + cw_ref[j:j + 1, cc] * ext_ref[off:off + C, cc]
        xa_ref[:, cc] = _silu(acc)

    xdt = dt_ref[...] + dtb_ref[...]
    dts = jnp.maximum(xdt, 0.0) + jnp.log(1.0 + jnp.exp(-jnp.abs(xdt)))
    log_a = dts * (-jnp.exp(alog_ref[...]))
    cum = _cumsum_rows(log_a, _tri(C))
    cum_t = cum.T
    cum_last = cum[C - 1:C, :]
    e_cum = jnp.exp(cum)
    w_end = jnp.exp(cum_last - cum)
    e_last = jnp.exp(cum_last)
    ci = lax.broadcasted_iota(jnp.int32, (C, C), 0)
    si = lax.broadcasted_iota(jnp.int32, (C, C), 1)
    causal = ci >= si

    for g in range(SSM_GROUPS):
        bm = xa_ref[:, xw + g * HEAD_DIM:xw + (g + 1) * HEAD_DIM].astype(BF16)
        cm = xa_ref[:, xw + (SSM_GROUPS + g) * HEAD_DIM:xw + (SSM_GROUPS + g + 1) * HEAD_DIM].astype(BF16)
        scores = _dot_nt(cm, bm)
        s_in = s_ref[g]
        inter = _dot(cm, s_in.astype(BF16))
        wv, scale = [], []
        for r in range(hpg):
            hd = g * hpg + r
            hc = slice(hd * SSM_HEADDIM, (hd + 1) * SSM_HEADDIM)
            xs = xa_ref[:, hc]
            v = xs * dts[:, hd:hd + 1]
            decay = jnp.where(causal, jnp.exp(cum[:, hd:hd + 1] - cum_t[hd:hd + 1, :]), 0.0)
            y = _dot((scores * decay).astype(BF16), v.astype(BF16))
            y = y + inter[:, r * SSM_HEADDIM:(r + 1) * SSM_HEADDIM] * e_cum[:, hd:hd + 1]
            y_ref[:, hc] = y + dsk_ref[:, hd:hd + 1] * xs
            wv.append(v * w_end[:, hd:hd + 1])
            scale.append(jnp.broadcast_to(e_last[:, hd:hd + 1], (1, SSM_HEADDIM)))
        upd = _dot_tn(bm, jnp.concatenate(wv, axis=1).astype(BF16))
        s_ref[g] = s_in * jnp.concatenate(scale, axis=1) + upd

    for g in range(SSM_GROUPS):
        gc = slice(g * gw, (g + 1) * gw)
        yz = y_ref[:, gc] * _silu(z_ref[:, gc].astype(F32))
        o_ref[:, gc] = _rms(yz, gn_ref[:, gc]).astype(o_ref.dtype)


def ssd_mixer(p16, g32, conv_w, conv_b, dt_bias, a_log, d_skip, gain, batch, seq, chunk, cb_xbc, cb_z, cb_dt):
    n = p16.shape[0]
    xw = SSM_HEADS * SSM_HEADDIM
    cw = conv_w.shape[1]
    chunk = min(chunk, seq)
    tps = seq // chunk
    row = lambda b, t: b * tps + t
    padl = lambda a: jnp.pad(a.reshape(1, -1), ((0, 0), (0, LANES - a.shape[-1])))
    const = lambda shape: pl.BlockSpec(shape, lambda b, t: (0, 0))
    return pl.pallas_call(
        _ssd_kernel,
        grid=(batch, tps),
        in_specs=[pl.BlockSpec((chunk, cw), lambda b, t: (row(b, t), cb_xbc)),
                  pl.BlockSpec((chunk, xw), lambda b, t: (row(b, t), cb_z)),
                  pl.BlockSpec((chunk, LANES), lambda b, t: (row(b, t), cb_dt)),
                  const(conv_w.shape), const((1, cw)), const((1, LANES)), const((1, LANES)),
                  const((1, LANES)), const((1, xw))],
        out_specs=pl.BlockSpec((chunk, xw), lambda b, t: (row(b, t), 0)),
        out_shape=jax.ShapeDtypeStruct((n, xw), BF16),
        scratch_shapes=[pltpu.VMEM((chunk + SUBLANES, cw), F32),
                        pltpu.VMEM((SUBLANES, cw), F32),
                        pltpu.VMEM((chunk, cw), F32),
                        pltpu.VMEM((SSM_GROUPS, HEAD_DIM, xw // SSM_GROUPS), F32),
                        pltpu.VMEM((chunk, xw), F32)],
        compiler_params=_params("arbitrary", "arbitrary"),
        name="ssd_mixer",
    )(p16, p16, g32, conv_w, conv_b.reshape(1, cw), padl(dt_bias), padl(a_log), padl(d_skip),
      gain.reshape(1, xw))


def _xattn_kernel(x_ref, g_ref, wq_ref, kv_ref, wo_ref, o_ref, *, nheads):
    x = x_ref[...]
    aw = nheads * HEAD_DIM
    q = _dot(_rms(x, g_ref[...]).astype(BF16), wq_ref[...]).astype(BF16)
    outs = []
    for h in range(nheads):
        cs = slice(h * HEAD_DIM, (h + 1) * HEAD_DIM)
        s = _dot_nt(q[:, cs], kv_ref[:, cs]) * (HEAD_DIM ** -0.5)
        e = jnp.exp(s - jnp.max(s, axis=-1, keepdims=True))
        o = _dot(e.astype(BF16), kv_ref[:, aw + h * HEAD_DIM:aw + (h + 1) * HEAD_DIM])
        outs.append((o / jnp.sum(e, axis=-1, keepdims=True)).astype(BF16))
    o_ref[...] = x + _dot(jnp.concatenate(outs, axis=1), wo_ref[...])


def cross_attention(x, gain, wq, kv, wo, seq, n_mem, tm):
    n, d = x.shape
    aw = wq.shape[1]
    tm = min(tm, seq)
    tps = seq // tm
    return pl.pallas_call(
        functools.partial(_xattn_kernel, nheads=XA_HEADS),
        grid=(n // tm,),
        in_specs=[pl.BlockSpec((tm, d), lambda i: (i, 0)),
                  pl.BlockSpec((1, d), lambda i: (0, 0)),
                  pl.BlockSpec((d, aw), lambda i: (0, 0)),
                  pl.BlockSpec((n_mem, 2 * aw), lambda i: (i // tps, 0)),
                  pl.BlockSpec((aw, d), lambda i: (0, 0))],
        out_specs=pl.BlockSpec((tm, d), lambda i: (i, 0)),
        out_shape=jax.ShapeDtypeStruct((n, d), F32),
        compiler_params=_params("arbitrary"),
        name="cross_attention",
    )(x, gain.reshape(1, d), wq, kv, wo)


def _ffn_kernel(x_ref, g_ref, wg_ref, wu_ref, cwg_ref, cwu_ref, cbg_ref, cbu_ref, wd_ref, o_ref,
                h_ref, ext_ref, tail_ref, *, tiles_per_seq):
    i, j = pl.program_id(0), pl.program_id(1)
    tm = x_ref.shape[0]
    kconv = cwg_ref.shape[0]
    pad = SUBLANES

    @pl.when(j == 0)
    def _():
        x = x_ref[...]
        h_ref[...] = _rms(x, g_ref[...]).astype(BF16)
        o_ref[...] = x

    first = (i % tiles_per_seq) == 0
    h = h_ref[...]

    def conv(u, cw_ref, cb_ref, slot):
        @pl.when(first)
        def _():
            ext_ref[slot, 0:pad, :] = jnp.zeros((pad, u.shape[1]), F32)

        @pl.when(jnp.logical_not(first))
        def _():
            ext_ref[slot, 0:pad, :] = tail_ref[slot, j]

        ext_ref[slot, pad:pad + tm, :] = u
        tail_ref[slot, j] = u[tm - pad:tm]
        acc = cb_ref[...] + cw_ref[kconv - 1:kconv, :] * u
        for t in range(kconv - 1):
            off = pad - (kconv - 1) + t
            acc = acc + cw_ref[t:t + 1, :] * ext_ref[slot, off:off + tm, :]
        return acc

    gate = conv(_dot(h, wg_ref[...]), cwg_ref, cbg_ref, 0)
    up = conv(_dot(h, wu_ref[...]), cwu_ref, cbu_ref, 1)
    o_ref[...] += _dot((_silu(gate) * up).astype(BF16), wd_ref[...])


def conv_glu_ffn(x, gain, w_up, conv_w, conv_b, w_down, seq, tm, tf):
    n, d = x.shape
    f = w_down.shape[0]
    tm, tf = min(tm, seq), min(tf, f)
    nf = f // tf
    kconv = conv_w.shape[0]
    conv_b = conv_b.reshape(1, 2 * f)
    return pl.pallas_call(
        functools.partial(_ffn_kernel, tiles_per_seq=seq // tm),
        grid=(n // tm, nf),
        in_specs=[pl.BlockSpec((tm, d), lambda i, j: (i, 0)),
                  pl.BlockSpec((1, d), lambda i, j: (0, 0)),
                  pl.BlockSpec((d, tf), lambda i, j: (0, j)),
                  pl.BlockSpec((d, tf), lambda i, j: (0, nf + j)),
                  pl.BlockSpec((kconv, tf), lambda i, j: (0, j)),
                  pl.BlockSpec((kconv, tf), lambda i, j: (0, nf + j)),
                  pl.BlockSpec((1, tf), lambda i, j: (0, j)),
                  pl.BlockSpec((1, tf), lambda i, j: (0, nf + j)),
                  pl.BlockSpec((tf, d), lambda i, j: (j, 0))],
        out_specs=pl.BlockSpec((tm, d), lambda i, j: (i, 0)),
        out_shape=jax.ShapeDtypeStruct((n, d), F32),
        scratch_shapes=[pltpu.VMEM((tm, d), BF16),
                        pltpu.VMEM((2, tm + SUBLANES, tf), F32),
                        pltpu.VMEM((2, nf, SUBLANES, tf), F32)],
        compiler_params=_params("arbitrary", "arbitrary"),
        name="conv_glu_ffn",
    )(x, gain.reshape(1, d), w_up, w_up, conv_w, conv_w, conv_b, conv_b, w_down)


def _rmsnorm_kernel(x_ref, g_ref, o_ref):
    o_ref[...] = _rms(x_ref[...], g_ref[...])


def rmsnorm(x, gain, tm):
    n, d = x.shape
    tm = min(tm, n)
    return pl.pallas_call(
        _rmsnorm_kernel,
        grid=(n // tm,),
        in_specs=[pl.BlockSpec((tm, d), lambda i: (i, 0)), pl.BlockSpec((1, d), lambda i: (0, 0))],
        out_specs=pl.BlockSpec((tm, d), lambda i: (i, 0)),
        out_shape=jax.ShapeDtypeStruct((n, d), F32),
        compiler_params=_params("arbitrary"),
        name="final_rmsnorm",
    )(x, gain.reshape(1, d))


def _rotary_tables(seq):
    half = HEAD_DIM // 2
    inv = ROPE_BASE ** (-jnp.arange(half, dtype=F32) / half)
    ang = jnp.arange(seq, dtype=F32)[:, None] * inv[None, :]
    cos, sin = jnp.cos(ang), jnp.sin(ang)
    return jnp.concatenate([cos, cos], axis=1), jnp.concatenate([-sin, sin], axis=1)


def _split_w_in(w_in):
    hgw, rtw, xw = HG_HEADS * HEAD_DIM, RET_HEADS * HEAD_DIM, SSM_HEADS * SSM_HEADDIM
    cw = xw + 2 * SSM_GROUPS * HEAD_DIM
    sizes = (hgw, hgw, hgw, hgw, rtw, rtw, rtw, rtw, xw, cw, SSM_HEADS)
    offs = [0]
    for s in sizes:
        offs.append(offs[-1] + s)
    hq, hf, hi, hg, rq, rk, rv, rg, z, xbc, dt = (w_in[..., offs[t]:offs[t + 1]] for t in range(len(sizes)))
    main = jnp.concatenate([xbc, z, hq, hi, hg, rq, rk, rv, rg], axis=-1).astype(BF16)
    gate = jnp.concatenate([hf, dt], axis=-1)
    gate = jnp.pad(gate, ((0, 0), (0, 0), (0, (-gate.shape[-1]) % LANES))).astype(BF16)
    return main, gate


def kernel(x, mem, w_in, w_out, hg_lb_logits, hg_norm, ret_norm, ssm_conv_w, ssm_conv_b, ssm_dt_bias, ssm_A_log, ssm_D, ssm_norm, norm_mix, norm_xattn, norm_mem, xa_wq, xa_wkv, xa_wo, norm_ffn, ffn_w_up, ffn_conv_w, ffn_conv_b, ffn_w_down, norm_final):
    batch, seq, d = x.shape
    n_mem = mem.shape[1]
    depth = w_in.shape[0]
    hgw = HG_HEADS * HEAD_DIM
    xw = SSM_HEADS * SSM_HEADDIM
    cw = ssm_conv_w.shape[-1]

    w_main, w_gate = _split_w_in(w_in)
    w_out16, wq16, wkv16, wo16 = (a.astype(BF16) for a in (w_out, xa_wq, xa_wkv, xa_wo))
    w_up16, w_down16 = ffn_w_up.astype(BF16), ffn_w_down.astype(BF16)
    p = jax.nn.softmax(hg_lb_logits.astype(F32), axis=0)
    lower_bounds = jnp.cumsum(p, axis=0) - p[0]
    cosf, sinf = _rotary_tables(seq)

    base = (cw + xw) // hgw
    cb_q, cb_i, cb_g, cb_rq, cb_rk, cb_rv, cb_rg = (base + t for t in range(7))

    xf = x.reshape(batch * seq, d)
    memf = mem.reshape(batch * n_mem, d)
    for l in range(depth):
        p16 = norm_matmul(xf, norm_mix[l], w_main[l], BF16, 1024, 512)
        g32 = norm_matmul(xf, norm_mix[l], w_gate[l], F32, 1024, w_gate.shape[-1])
        o_hg = hgrn_mixer(p16, g32, lower_bounds[l], hg_norm[l], batch, seq, 256, cb_q, cb_i, cb_g)
        o_rt = ret_mixer(p16, cosf, sinf, ret_norm[l], batch, seq, 256, cb_rq, cb_rk, cb_rv, cb_rg)
        o_ss = ssd_mixer(p16, g32, ssm_conv_w[l], ssm_conv_b[l], ssm_dt_bias[l], ssm_A_log[l], ssm_D[l],
                         ssm_norm[l], batch, seq, 256, 0, cw // xw, hgw // LANES)
        xf = resid_matmul(xf, o_hg, o_rt, o_ss, w_out16[l], 512)
        kv = norm_matmul(memf, norm_mem[l], wkv16[l], BF16, n_mem, wkv16.shape[-1])
        xf = cross_attention(xf, norm_xattn[l], wq16[l], kv, wo16[l], seq, n_mem, 512)
        xf = conv_glu_ffn(xf, norm_ffn[l], w_up16[l], ffn_conv_w[l], ffn_conv_b[l], w_down16[l], seq, 512, 512)
    return rmsnorm(xf, norm_final, 512).reshape(batch, seq, d)
```

```python
import functools
import math

import jax
import jax.numpy as jnp
from jax import lax
from jax.experimental import pallas as pl
from jax.experimental.pallas import tpu as pltpu

F32 = jnp.float32
BF16 = jnp.bfloat16
EPS = 1e-6
LANES = 128
SUBLANES = 8
VMEM_LIMIT_BYTES = 56 * 1024 * 1024

HEAD_DIM = 128
HG_HEADS = 4
RET_HEADS = 4
SSM_HEADS = 16
SSM_HEADDIM = 64
SSM_GROUPS = 4
XA_HEADS = 4
ROPE_BASE = 10000.0
HG_CHUNK = 64
HG_SUB = 16


def _params(*sem):
    return pltpu.CompilerParams(dimension_semantics=sem, vmem_limit_bytes=VMEM_LIMIT_BYTES)


def _rms(x, gain):
    ms = jnp.mean(x * x, axis=-1, keepdims=True)
    return x * lax.rsqrt(ms + EPS) * gain


def _dot(a, b):
    return jnp.dot(a, b, preferred_element_type=F32)


def _dot_nt(a, b):
    return lax.dot_general(a, b, (((1,), (1,)), ((), ())), preferred_element_type=F32)


def _dot_tn(a, b):
    return lax.dot_general(a, b, (((0,), (0,)), ((), ())), preferred_element_type=F32)


def _sigmoid(x):
    return 1.0 / (1.0 + jnp.exp(-x))


def _silu(x):
    return x * _sigmoid(x)


def _cumsum_rows(x, tri):
    hi = x.astype(BF16)
    lo = (x - hi.astype(F32)).astype(BF16)
    return _dot(tri, hi) + _dot(tri, lo)


def _tri(n):
    r = lax.broadcasted_iota(jnp.int32, (n, n), 0)
    c = lax.broadcasted_iota(jnp.int32, (n, n), 1)
    return (r >= c).astype(BF16)


def _norm_matmul_kernel(x_ref, g_ref, w_ref, o_ref, h_ref):
    @pl.when(pl.program_id(1) == 0)
    def _():
        h_ref[...] = _rms(x_ref[...], g_ref[...]).astype(BF16)

    o_ref[...] = _dot(h_ref[...], w_ref[...]).astype(o_ref.dtype)


def norm_matmul(x, gain, w, out_dtype, tm, tn):
    n, d = x.shape
    nout = w.shape[1]
    tm, tn = min(tm, n), min(tn, nout)
    return pl.pallas_call(
        _norm_matmul_kernel,
        grid=(n // tm, nout // tn),
        in_specs=[pl.BlockSpec((tm, d), lambda i, j: (i, 0)),
                  pl.BlockSpec((1, d), lambda i, j: (0, 0)),
                  pl.BlockSpec((d, tn), lambda i, j: (0, j))],
        out_specs=pl.BlockSpec((tm, tn), lambda i, j: (i, j)),
        out_shape=jax.ShapeDtypeStruct((n, nout), out_dtype),
        scratch_shapes=[pltpu.VMEM((tm, d), BF16)],
        compiler_params=_params("arbitrary", "arbitrary"),
        name="norm_matmul",
    )(x, gain.reshape(1, d), w)


def _resid_matmul_kernel(x_ref, a0_ref, a1_ref, a2_ref, w_ref, o_ref):
    k0, k1 = a0_ref.shape[1], a1_ref.shape[1]
    acc = x_ref[...] + _dot(a0_ref[...], w_ref[0:k0, :])
    acc = acc + _dot(a1_ref[...], w_ref[k0:k0 + k1, :])
    acc = acc + _dot(a2_ref[...], w_ref[k0 + k1:, :])
    o_ref[...] = acc


def resid_matmul(x, a0, a1, a2, w, tm):
    n, d = x.shape
    tm = min(tm, n)
    ktot = w.shape[0]
    acts = (a0, a1, a2)
    return pl.pallas_call(
        _resid_matmul_kernel,
        grid=(n // tm,),
        in_specs=[pl.BlockSpec((tm, d), lambda i: (i, 0))]
        + [pl.BlockSpec((tm, a.shape[1]), lambda i: (i, 0)) for a in acts]
        + [pl.BlockSpec((ktot, d), lambda i: (0, 0))],
        out_specs=pl.BlockSpec((tm, d), lambda i: (i, 0)),
        out_shape=jax.ShapeDtypeStruct((n, d), F32),
        compiler_params=_params("arbitrary"),
        name="mix_out_proj",
    )(x, a0, a1, a2, w)


def _hgrn_kernel(hq_ref, hi_ref, hg_ref, hf_ref, lb_ref, gn_ref, o_ref, st_ref, y_ref, *, nheads):
    C, SB = HG_CHUNK, HG_SUB
    nsub = C // SB
    hs = SB // 2

    @pl.when(pl.program_id(1) == 0)
    def _():
        st_ref[...] = jnp.zeros_like(st_ref)

    nchunks = hq_ref.shape[0] // C
    tri = _tri(C)
    ones = jnp.ones((LANES, LANES), BF16)
    row_full = lax.broadcasted_iota(jnp.int32, (SB, LANES), 0)
    row_half = lax.broadcasted_iota(jnp.int32, (hs, LANES), 0) + hs
    zero_half = jnp.zeros((hs, LANES), F32)

    def head_chunk(rows, h):
        cs = slice(h * HEAD_DIM, (h + 1) * HEAD_DIM)
        lb = lb_ref[:, cs]
        st = st_ref[h]
        hf = hf_ref[rows, cs]
        log_f = jnp.log(lb + (1.0 - lb) * _sigmoid(hf))
        k = (1.0 - lb) * _sigmoid(-hf)
        q = _silu(hq_ref[rows, cs].astype(F32))
        v = hi_ref[rows, cs].astype(F32)
        v16 = v.astype(BF16)
        b = _cumsum_rows(log_f, tri)
        b_last = b[C - 1:C, :]

        o = _dot_nt((q * jnp.exp(b)).astype(BF16), st.astype(BF16))
        kw = k * jnp.exp(b_last - b)
        st_ref[h] = st * jnp.exp(b_last) + _dot_tn(v16, kw.astype(BF16))

        def off_block(q0, q1, k0, k1):
            ref = b[q0 - 1:q0, :]
            qt = q[q0:q1] * jnp.exp(b[q0:q1] - ref)
            kt = k[k0:k1] * jnp.exp(ref - b[k0:k1])
            a = _dot_nt(qt.astype(BF16), kt.astype(BF16))
            return _dot(a.astype(BF16), v16[k0:k1])

        half = C // 2
        o_lo = [jnp.zeros((SB, HEAD_DIM), F32), off_block(SB, 2 * SB, 0, SB)]
        o_hi = off_block(half, C, 0, half)
        o_hi = [o_hi[0:SB], o_hi[SB:2 * SB] + off_block(3 * SB, C, half, 3 * SB)]
        o = o + jnp.concatenate(o_lo + o_hi, axis=0)

        for j in range(nsub):
            r0 = j * SB
            for s in range(SB):
                c0 = r0 if s < hs else r0 + hs
                rowi = row_full if s < hs else row_half
                e = jnp.exp(jnp.minimum(b[c0:r0 + SB] - b[r0 + s:r0 + s + 1, :], 0.0))
                y = jnp.where(rowi >= s, q[c0:r0 + SB] * k[r0 + s:r0 + s + 1, :] * e, 0.0)
                if s >= hs:
                    y = jnp.concatenate([zero_half, y], axis=0)
                y_ref[h, pl.ds((r0 + s) * SB, SB), :] = y.astype(BF16)
        red = _dot(y_ref[h], ones)
        o_diag = []
        for j in range(nsub):
            r0 = j * SB
            top = jnp.zeros((hs, HEAD_DIM), F32)
            bot = jnp.zeros((hs, HEAD_DIM), F32)
            for s in range(SB):
                base = (r0 + s) * SB
                vs = v[r0 + s:r0 + s + 1, :]
                if s < hs:
                    top = top + red[base:base + hs, :] * vs
                bot = bot + red[base + hs:base + SB, :] * vs
            o_diag += [top, bot]
        o = o + jnp.concatenate(o_diag, axis=0)

        g = hg_ref[rows, cs].astype(F32)
        o_ref[rows, cs] = (_rms(o, gn_ref[:, cs]) * _sigmoid(g)).astype(o_ref.dtype)

    def body(n, carry):
        rows = pl.ds(pl.multiple_of(n * C, C), C)
        for h in range(nheads):
            head_chunk(rows, h)
        return carry

    lax.fori_loop(0, nchunks, body, 0)


def hgrn_mixer(p16, g32, lb, gain, batch, seq, tt, cb_q, cb_i, cb_g):
    n = p16.shape[0]
    w = HG_HEADS * HEAD_DIM
    tt = min(tt, seq)
    tps = seq // tt
    row = lambda b, t: b * tps + t
    return pl.pallas_call(
        functools.partial(_hgrn_kernel, nheads=HG_HEADS),
        grid=(batch, tps),
        in_specs=[pl.BlockSpec((tt, w), lambda b, t: (row(b, t), cb_q)),
                  pl.BlockSpec((tt, w), lambda b, t: (row(b, t), cb_i)),
                  pl.BlockSpec((tt, w), lambda b, t: (row(b, t), cb_g)),
                  pl.BlockSpec((tt, w), lambda b, t: (row(b, t), 0)),
                  pl.BlockSpec((1, w), lambda b, t: (0, 0)),
                  pl.BlockSpec((1, w), lambda b, t: (0, 0))],
        out_specs=pl.BlockSpec((tt, w), lambda b, t: (row(b, t), 0)),
        out_shape=jax.ShapeDtypeStruct((n, w), BF16),
        scratch_shapes=[pltpu.VMEM((HG_HEADS, HEAD_DIM, HEAD_DIM), F32),
                        pltpu.VMEM((HG_HEADS, HG_CHUNK * HG_SUB, LANES), BF16)],
        compiler_params=_params("arbitrary", "arbitrary"),
        name="hgrn_mixer",
    )(p16, p16, p16, g32, lb.reshape(1, w), gain.reshape(1, w))


def _ret_kernel(q_ref, k_ref, v_ref, g_ref, cos_ref, sin_ref, gn_ref, o_ref, s_ref, *, nheads):
    @pl.when(pl.program_id(1) == 0)
    def _():
        s_ref[...] = jnp.zeros_like(s_ref)

    C = q_ref.shape[0]
    ci = lax.broadcasted_iota(jnp.int32, (C, C), 0)
    si = lax.broadcasted_iota(jnp.int32, (C, C), 1)
    causal = ci >= si
    dpos = (ci - si).astype(F32)
    rowpos = lax.broadcasted_iota(jnp.int32, (C, HEAD_DIM), 0).astype(F32)
    cosf = cos_ref[...]
    sinf = sin_ref[...]

    def rot(x):
        return x * cosf + pltpu.roll(x, HEAD_DIM // 2, 1) * sinf

    for h in range(nheads):
        lg = math.log(1.0 - 2.0 ** (-5.0 - h))
        cs = slice(h * HEAD_DIM, (h + 1) * HEAD_DIM)
        qr = rot(q_ref[:, cs].astype(F32))
        kr = rot(k_ref[:, cs].astype(F32)) * (HEAD_DIM ** -0.5)
        v = v_ref[:, cs]
        scores = _dot_nt(qr.astype(BF16), kr.astype(BF16))
        decay = jnp.where(causal, jnp.exp(dpos * lg), 0.0)
        o = _dot((scores * decay).astype(BF16), v)
        s_in = s_ref[h]
        o = o + _dot((qr * jnp.exp((rowpos + 1.0) * lg)).astype(BF16), s_in.astype(BF16))
        kw = kr * jnp.exp((C - 1.0 - rowpos) * lg)
        s_ref[h] = s_in * math.exp(C * lg) + _dot_tn(kw.astype(BF16), v)
        g = g_ref[:, cs].astype(F32)
        o_ref[:, cs] = (_rms(o, gn_ref[:, cs]) * _silu(g)).astype(o_ref.dtype)


def ret_mixer(p16, cosf, sinf, gain, batch, seq, chunk, cb_q, cb_k, cb_v, cb_g):
    n = p16.shape[0]
    w = RET_HEADS * HEAD_DIM
    chunk = min(chunk, seq)
    tps = seq // chunk
    row = lambda b, t: b * tps + t
    return pl.pallas_call(
        functools.partial(_ret_kernel, nheads=RET_HEADS),
        grid=(batch, tps),
        in_specs=[pl.BlockSpec((chunk, w), lambda b, t: (row(b, t), cb_q)),
                  pl.BlockSpec((chunk, w), lambda b, t: (row(b, t), cb_k)),
                  pl.BlockSpec((chunk, w), lambda b, t: (row(b, t), cb_v)),
                  pl.BlockSpec((chunk, w), lambda b, t: (row(b, t), cb_g)),
                  pl.BlockSpec((chunk, HEAD_DIM), lambda b, t: (t, 0)),
                  pl.BlockSpec((chunk, HEAD_DIM), lambda b, t: (t, 0)),
                  pl.BlockSpec((1, w), lambda b, t: (0, 0))],
        out_specs=pl.BlockSpec((chunk, w), lambda b, t: (row(b, t), 0)),
        out_shape=jax.ShapeDtypeStruct((n, w), BF16),
        scratch_shapes=[pltpu.VMEM((RET_HEADS, HEAD_DIM, HEAD_DIM), F32)],
        compiler_params=_params("arbitrary", "arbitrary"),
        name="ret_mixer",
    )(p16, p16, p16, p16, cosf, sinf, gain.reshape(1, w))


def _ssd_kernel(xbc_ref, z_ref, dt_ref, cw_ref, cb_ref, dtb_ref, alog_ref, dsk_ref, gn_ref, o_ref,
                ext_ref, tail_ref, xa_ref, s_ref, y_ref):
    C, W = xbc_ref.shape
    kconv = cw_ref.shape[0]
    hpg = SSM_HEADS // SSM_GROUPS
    xw = SSM_HEADS * SSM_HEADDIM
    gw = hpg * SSM_HEADDIM
    pad = SUBLANES

    @pl.when(pl.program_id(1) == 0)
    def _():
        tail_ref[...] = jnp.zeros_like(tail_ref)
        s_ref[...] = jnp.zeros_like(s_ref)

    ext_ref[0:pad, :] = tail_ref[...]
    ext_ref[pad:pad + C, :] = xbc_ref[...].astype(F32)
    tail_ref[...] = ext_ref[C:C + pad, :]
    cblk = 4 * LANES
    for c0 in range(0, W, cblk):
        cc = slice(c0, c0 + cblk)
        acc = cb_ref[:, cc] + cw_ref[kconv - 1:kconv, cc] * ext_ref[pad:pad + C, cc]
        for j in range(kconv - 1):
            off = pad - (kconv - 1) + j
            acc = acc + cw_ref[j:j + 1, cc] * ext_ref[off:off + C, cc]
        xa_ref[:, cc] = _silu(acc)

    xdt = dt_ref[...] + dtb_ref[...]
    dts = jnp.maximum(xdt, 0.0) + jnp.log(1.0 + jnp.exp(-jnp.abs(xdt)))
    log_a = dts * (-jnp.exp(alog_ref[...]))
    cum = _cumsum_rows(log_a, _tri(C))
    cum_t = cum.T
    cum_last = cum[C - 1:C, :]
    e_cum = jnp.exp(cum)
    w_end = jnp.exp(cum_last - cum)
    e_last = jnp.exp(cum_last)
    ci = lax.broadcasted_iota(jnp.int32, (C, C), 0)
    si = lax.broadcasted_iota(jnp.int32, (C, C), 1)
    causal = ci >= si

    for g in range(SSM_GROUPS):
        bm = xa_ref[:, xw + g * HEAD_DIM:xw + (g + 1) * HEAD_DIM].astype(BF16)
        cm = xa_ref[:, xw + (SSM_GROUPS + g) * HEAD_DIM:xw + (SSM_GROUPS + g + 1) * HEAD_DIM].astype(BF16)
        scores = _dot_nt(cm, bm)
        s_in = s_ref[g]
        inter = _dot(cm, s_in.astype(BF16))
        wv, scale = [], []
        for r in range(hpg):
            hd = g * hpg + r
            hc = slice(hd * SSM_HEADDIM, (hd + 1) * SSM_HEADDIM)
            xs = xa_ref[:, hc]
            v = xs * dts[:, hd:hd + 1]
            decay = jnp.where(causal, jnp.exp(cum[:, hd:hd + 1] - cum_t[hd:hd + 1, :]), 0.0)
            y = _dot((scores * decay).astype(BF16), v.astype(BF16))
            y = y + inter[:, r * SSM_HEADDIM:(r + 1) * SSM_HEADDIM] * e_cum[:, hd:hd + 1]
            y_ref[:, hc] = y + dsk_ref[:, hd:hd + 1] * xs
            wv.append(v * w_end[:, hd:hd + 1])
            scale.append(jnp.broadcast_to(e_last[:, hd:hd + 1], (1, SSM_HEADDIM)))
        upd = _dot_tn(bm, jnp.concatenate(wv, axis=1).astype(BF16))
        s_ref[g] = s_in * jnp.concatenate(scale, axis=1) + upd

    for g in range(SSM_GROUPS):
        gc = slice(g * gw, (g + 1) * gw)
        yz = y_ref[:, gc] * _silu(z_ref[:, gc].astype(F32))
        o_ref[:, gc] = _rms(yz, gn_ref[:, gc]).astype(o_ref.dtype)


def ssd_mixer(p16, g32, conv_w, conv_b, dt_bias, a_log, d_skip, gain, batch, seq, chunk, cb_xbc, cb_z, cb_dt):
    n = p16.shape[0]
    xw = SSM_HEADS * SSM_HEADDIM
    cw = conv_w.shape[1]
    chunk = min(chunk, seq)
    tps = seq // chunk
    row = lambda b, t: b * tps + t
    padl = lambda a: jnp.pad(a.reshape(1, -1), ((0, 0), (0, LANES - a.shape[-1])))
    const = lambda shape: pl.BlockSpec(shape, lambda b, t: (0, 0))
    return pl.pallas_call(
        _ssd_kernel,
        grid=(batch, tps),
        in_specs=[pl.BlockSpec((chunk, cw), lambda b, t: (row(b, t), cb_xbc)),
                  pl.BlockSpec((chunk, xw), lambda b, t: (row(b, t), cb_z)),
                  pl.BlockSpec((chunk, LANES), lambda b, t: (row(b, t), cb_dt)),
                  const(conv_w.shape), const((1, cw)), const((1, LANES)), const((1, LANES)),
                  const((1, LANES)), const((1, xw))],
        out_specs=pl.BlockSpec((chunk, xw), lambda b, t: (row(b, t), 0)),
        out_shape=jax.ShapeDtypeStruct((n, xw), BF16),
        scratch_shapes=[pltpu.VMEM((chunk + SUBLANES, cw), F32),
                        pltpu.VMEM((SUBLANES, cw), F32),
                        pltpu.VMEM((chunk, cw), F32),
                        pltpu.VMEM((SSM_GROUPS, HEAD_DIM, xw // SSM_GROUPS), F32),
                        pltpu.VMEM((chunk, xw), F32)],
        compiler_params=_params("arbitrary", "arbitrary"),
        name="ssd_mixer",
    )(p16, p16, g32, conv_w, conv_b.reshape(1, cw), padl(dt_bias), padl(a_log), padl(d_skip),
      gain.reshape(1, xw))


def _xattn_kernel(x_ref, g_ref, wq_ref, kv_ref, wo_ref, o_ref, *, nheads):
    x = x_ref[...]
    aw = nheads * HEAD_DIM
    q = _dot(_rms(x, g_ref[...]).astype(BF16), wq_ref[...]).astype(BF16)
    outs = []
    for h in range(nheads):
        cs = slice(h * HEAD_DIM, (h + 1) * HEAD_DIM)
        s = _dot_nt(q[:, cs], kv_ref[:, cs]) * (HEAD_DIM ** -0.5)
        e = jnp.exp(s - jnp.max(s, axis=-1, keepdims=True))
        o = _dot(e.astype(BF16), kv_ref[:, aw + h * HEAD_DIM:aw + (h + 1) * HEAD_DIM])
        outs.append((o / jnp.sum(e, axis=-1, keepdims=True)).astype(BF16))
    o_ref[...] = x + _dot(jnp.concatenate(outs, axis=1), wo_ref[...])


def cross_attention(x, gain, wq, kv, wo, seq, n_mem, tm):
    n, d = x.shape
    aw = wq.shape[1]
    tm = min(tm, seq)
    tps = seq // tm
    return pl.pallas_call(
        functools.partial(_xattn_kernel, nheads=XA_HEADS),
        grid=(n // tm,),
        in_specs=[pl.BlockSpec((tm, d), lambda i: (i, 0)),
                  pl.BlockSpec((1, d), lambda i: (0, 0)),
                  pl.BlockSpec((d, aw), lambda i: (0, 0)),
                  pl.BlockSpec((n_mem, 2 * aw), lambda i: (i // tps, 0)),
                  pl.BlockSpec((aw, d), lambda i: (0, 0))],
        out_specs=pl.BlockSpec((tm, d), lambda i: (i, 0)),
        out_shape=jax.ShapeDtypeStruct((n, d), F32),
        compiler_params=_params("arbitrary"),
        name="cross_attention",
    )(x, gain.reshape(1, d), wq, kv, wo)


def _ffn_kernel(x_ref, g_ref, wg_ref, wu_ref, cwg_ref, cwu_ref, cbg_ref, cbu_ref, wd_ref, o_ref,
                h_ref, u_ref, tail_ref, *, tiles_per_seq, nf):
    i, j = pl.program_id(0), pl.program_id(1)
    tm = x_ref.shape[0]
    kconv = cwg_ref.shape[0]
    pad = SUBLANES
    first = (i % tiles_per_seq) == 0

    def up_part(slot, jj):
        h = h_ref[...]
        for a, w_ref in ((0, wg_ref), (1, wu_ref)):
            u = _dot(h, w_ref[...])
            u_ref[slot, a, 0:pad, :] = jnp.where(first, 0.0, tail_ref[a, jj])
            u_ref[slot, a, pad:pad + tm, :] = u
            tail_ref[a, jj] = u[tm - pad:tm]

    def down_part(slot):
        def conv(a, cw_ref, cb_ref):
            acc = cb_ref[...] + cw_ref[kconv - 1:kconv, :] * u_ref[slot, a, pad:pad + tm, :]
            for t in range(kconv - 1):
                off = pad - (kconv - 1) + t
                acc = acc + cw_ref[t:t + 1, :] * u_ref[slot, a, off:off + tm, :]
            return acc

        act = _silu(conv(0, cwg_ref, cbg_ref)) * conv(1, cwu_ref, cbu_ref)
        o_ref[...] += _dot(act.astype(BF16), wd_ref[...])

    @pl.when(j == 0)
    def _():
        @pl.when(i == 0)
        def _():
            tail_ref[...] = jnp.zeros_like(tail_ref)

        x = x_ref[...]
        h_ref[...] = _rms(x, g_ref[...]).astype(BF16)
        o_ref[...] = x
        up_part(0, 0)

    @pl.when(jnp.logical_and(j > 0, j < nf))
    def _():
        down_part((j - 1) % 2)
        up_part(j % 2, j)

    @pl.when(j == nf)
    def _():
        down_part((nf - 1) % 2)


def conv_glu_ffn(x, gain, w_up, conv_w, conv_b, w_down, seq, tm, tf):
    n, d = x.shape
    f = w_down.shape[0]
    tm, tf = min(tm, seq), min(tf, f)
    nf = f // tf
    kconv = conv_w.shape[0]
    conv_b = conv_b.reshape(1, 2 * f)
    up_blk = lambda j: jnp.minimum(j, nf - 1)
    dn_blk = lambda j: jnp.maximum(j - 1, 0)
    return pl.pallas_call(
        functools.partial(_ffn_kernel, tiles_per_seq=seq // tm, nf=nf),
        grid=(n // tm, nf + 1),
        in_specs=[pl.BlockSpec((tm, d), lambda i, j: (i, 0)),
                  pl.BlockSpec((1, d), lambda i, j: (0, 0)),
                  pl.BlockSpec((d, tf), lambda i, j: (0, up_blk(j))),
                  pl.BlockSpec((d, tf), lambda i, j: (0, nf + up_blk(j))),
                  pl.BlockSpec((kconv, tf), lambda i, j: (0, dn_blk(j))),
                  pl.BlockSpec((kconv, tf), lambda i, j: (0, nf + dn_blk(j))),
                  pl.BlockSpec((1, tf), lambda i, j: (0, dn_blk(j))),
                  pl.BlockSpec((1, tf), lambda i, j: (0, nf + dn_blk(j))),
                  pl.BlockSpec((tf, d), lambda i, j: (dn_blk(j), 0))],
        out_specs=pl.BlockSpec((tm, d), lambda i, j: (i, 0)),
        out_shape=jax.ShapeDtypeStruct((n, d), F32),
        scratch_shapes=[pltpu.VMEM((tm, d), BF16),
                        pltpu.VMEM((2, 2, tm + SUBLANES, tf), F32),
                        pltpu.VMEM((2, nf, SUBLANES, tf), F32)],
        compiler_params=_params("arbitrary", "arbitrary"),
        name="conv_glu_ffn",
    )(x, gain.reshape(1, d), w_up, w_up, conv_w, conv_w, conv_b, conv_b, w_down)


def _rmsnorm_kernel(x_ref, g_ref, o_ref):
    o_ref[...] = _rms(x_ref[...], g_ref[...])


def rmsnorm(x, gain, tm):
    n, d = x.shape
    tm = min(tm, n)
    return pl.pallas_call(
        _rmsnorm_kernel,
        grid=(n // tm,),
        in_specs=[pl.BlockSpec((tm, d), lambda i: (i, 0)), pl.BlockSpec((1, d), lambda i: (0, 0))],
        out_specs=pl.BlockSpec((tm, d), lambda i: (i, 0)),
        out_shape=jax.ShapeDtypeStruct((n, d), F32),
        compiler_params=_params("arbitrary"),
        name="final_rmsnorm",
    )(x, gain.reshape(1, d))


def _rotary_tables(seq):
    half = HEAD_DIM // 2
    inv = ROPE_BASE ** (-jnp.arange(half, dtype=F32) / half)
    ang = jnp.arange(seq, dtype=F32)[:, None] * inv[None, :]
    cos, sin = jnp.cos(ang), jnp.sin(ang)
    return jnp.concatenate([cos, cos], axis=1), jnp.concatenate([-sin, sin], axis=1)


def _split_w_in(w_in):
    hgw, rtw, xw = HG_HEADS * HEAD_DIM, RET_HEADS * HEAD_DIM, SSM_HEADS * SSM_HEADDIM
    cw = xw + 2 * SSM_GROUPS * HEAD_DIM
    sizes = (hgw, hgw, hgw, hgw, rtw, rtw, rtw, rtw, xw, cw, SSM_HEADS)
    offs = [0]
    for s in sizes:
        offs.append(offs[-1] + s)
    hq, hf, hi, hg, rq, rk, rv, rg, z, xbc, dt = (w_in[..., offs[t]:offs[t + 1]] for t in range(len(sizes)))
    main = jnp.concatenate([xbc, z, hq, hi, hg, rq, rk, rv, rg], axis=-1).astype(BF16)
    gate = jnp.concatenate([hf, dt], axis=-1)
    gate = jnp.pad(gate, ((0, 0), (0, 0), (0, (-gate.shape[-1]) % LANES))).astype(BF16)
    return main, gate


def kernel(x, mem, w_in, w_out, hg_lb_logits, hg_norm, ret_norm, ssm_conv_w, ssm_conv_b, ssm_dt_bias, ssm_A_log, ssm_D, ssm_norm, norm_mix, norm_xattn, norm_mem, xa_wq, xa_wkv, xa_wo, norm_ffn, ffn_w_up, ffn_conv_w, ffn_conv_b, ffn_w_down, norm_final):
    batch, seq, d = x.shape
    n_mem = mem.shape[1]
    depth = w_in.shape[0]
    hgw = HG_HEADS * HEAD_DIM
    xw = SSM_HEADS * SSM_HEADDIM
    cw = ssm_conv_w.shape[-1]

    w_main, w_gate = _split_w_in(w_in)
    w_out16, wq16, wkv16, wo16 = (a.astype(BF16) for a in (w_out, xa_wq, xa_wkv, xa_wo))
    w_up16, w_down16 = ffn_w_up.astype(BF16), ffn_w_down.astype(BF16)
    p = jax.nn.softmax(hg_lb_logits.astype(F32), axis=0)
    lower_bounds = jnp.cumsum(p, axis=0) - p[0]
    cosf, sinf = _rotary_tables(seq)

    base = (cw + xw) // hgw
    cb_q, cb_i, cb_g, cb_rq, cb_rk, cb_rv, cb_rg = (base + t for t in range(7))

    xf = x.reshape(batch * seq, d)
    memf = mem.reshape(batch * n_mem, d)
    for l in range(depth):
        p16 = norm_matmul(xf, norm_mix[l], w_main[l], BF16, 1024, 512)
        g32 = norm_matmul(xf, norm_mix[l], w_gate[l], F32, 1024, w_gate.shape[-1])
        o_hg = hgrn_mixer(p16, g32, lower_bounds[l], hg_norm[l], batch, seq, 1024, cb_q, cb_i, cb_g)
        o_rt = ret_mixer(p16, cosf, sinf, ret_norm[l], batch, seq, 256, cb_rq, cb_rk, cb_rv, cb_rg)
        o_ss = ssd_mixer(p16, g32, ssm_conv_w[l], ssm_conv_b[l], ssm_dt_bias[l], ssm_A_log[l], ssm_D[l],
                         ssm_norm[l], batch, seq, 256, 0, cw // xw, hgw // LANES)
        xf = resid_matmul(xf, o_hg, o_rt, o_ss, w_out16[l], 512)
        kv = norm_matmul(memf, norm_mem[l], wkv16[l], BF16, n_mem, wkv16.shape[-1])
        xf = cross_attention(xf, norm_xattn[l], wq16[l], kv, wo16[l], seq, n_mem, 512)
        xf = conv_glu_ffn(xf, norm_ffn[l], w_up16[l], ffn_conv_w[l], ffn_conv_b[l], w_down16[l], seq, 512, 512)
    return rmsnorm(xf, norm_final, 512).reshape(batch, seq, d)
```
